```python
import jax, jax.numpy as jnp
from jax import lax
import numpy as np

D_MODEL = 1024
BATCH = 4
SEQ = 4096
DEPTH = 2

CHUNK = 64
Q_BLOCK = 128
EPS = 1e-6

A_HEADS = 8
A_NOPE = 64
A_ROPE = 32
A_VDIM = 64
A_QLORA = 256
A_KVLORA = 128
ROPE_THETA = 10000.0

B_HEADS = 4
B_DIM = 64

POOL_WINDOWS = (2, 4, 8, 16)
POOL_CH = 64

A_WIDTH = A_HEADS * A_VDIM
B_WIDTH = B_HEADS * B_DIM
C_WIDTH = len(POOL_WINDOWS) * POOL_CH
D_MIX = A_WIDTH + B_WIDTH + C_WIDTH

A_IN = A_QLORA + A_KVLORA + A_ROPE
B_IN = 3 * B_WIDTH
D_IN = A_IN + B_IN + C_WIDTH

N_GROUPS = 4
EXPERTS_PER_GROUP = 8
TOP_K_INNER = 2
D_EXPERT = 256

kernel_name = 'hybrid_mla_stickbreak_pool_hmoe'


def rms_norm(x, g):
    xf = x.astype(jnp.float32)
    y = xf * lax.rsqrt(jnp.mean(xf * xf, axis=-1, keepdims=True) + EPS)
    return (y * g.astype(jnp.float32)).astype(x.dtype)


def rope_tables(seq):
    pos = jnp.arange(seq, dtype=jnp.float32)
    inv = ROPE_THETA ** (-jnp.arange(0, A_ROPE, 2, dtype=jnp.float32) / A_ROPE)
    ang = pos[:, None] * inv[None, :]
    return jnp.cos(ang), jnp.sin(ang)


def apply_rope(x, cos, sin):
    half = x.shape[-1] // 2
    x1 = x[..., :half].astype(jnp.float32)
    x2 = x[..., half:].astype(jnp.float32)
    out = jnp.concatenate([x1 * cos - x2 * sin, x2 * cos + x1 * sin], axis=-1)
    return out.astype(x.dtype)


def mla_mixer(c_q, c_kv, k_r, q_norm_g, w_uq, kv_norm_g, w_ukv):
    B, S, _ = c_q.shape
    cos, sin = rope_tables(S)
    q = jnp.einsum('bsr,rf->bsf', rms_norm(c_q, q_norm_g), w_uq).reshape(B, S, A_HEADS, A_NOPE + A_ROPE)
    q = jnp.concatenate([q[..., :A_NOPE], apply_rope(q[..., A_NOPE:], cos[None, :, None], sin[None, :, None])], axis=-1)
    kv = jnp.einsum('bsr,rf->bsf', rms_norm(c_kv, kv_norm_g), w_ukv).reshape(B, S, A_HEADS, A_NOPE + A_VDIM)
    k_rope = apply_rope(k_r, cos[None], sin[None])
    k = jnp.concatenate([kv[..., :A_NOPE], jnp.broadcast_to(k_rope[:, :, None, :], (B, S, A_HEADS, A_ROPE))], axis=-1)
    v = kv[..., A_NOPE:]
    scale = (A_NOPE + A_ROPE) ** -0.5
    chunk_id = jnp.arange(S) // CHUNK
    outs = []
    for i in range(S // Q_BLOCK):
        q0, q1 = i * Q_BLOCK, (i + 1) * Q_BLOCK
        s = jnp.einsum('bqhd,bkhd->bhqk', q[:, q0:q1], k[:, :q1]).astype(jnp.float32) * scale
        allowed = chunk_id[:q1][None, :] <= chunk_id[q0:q1][:, None]
        s = jnp.where(allowed, s, -1e30)
        p = jax.nn.softmax(s, axis=-1).astype(v.dtype)
        outs.append(jnp.einsum('bhqk,bkhd->bqhd', p, v[:, :q1]))
    return jnp.concatenate(outs, axis=1).reshape(B, S, A_WIDTH)


def stick_breaking_mixer(q, k, v):
    B, S = q.shape[0], q.shape[1]
    pos = jnp.arange(S)
    scale = B_DIM ** -0.5
    outs = []
    for i in range(S // Q_BLOCK):
        q0, q1 = i * Q_BLOCK, (i + 1) * Q_BLOCK
        z = jnp.einsum('bqhd,bkhd->bhqk', q[:, q0:q1], k[:, :q1]).astype(jnp.float32) * scale
        strict = pos[:q1][None, :] < pos[q0:q1][:, None]
        log_1m_beta = jnp.where(strict, -jax.nn.softplus(z), 0.0)
        between = lax.cumsum(log_1m_beta, axis=3, reverse=True) - log_1m_beta
        a = jnp.where(strict, jnp.exp(jax.nn.log_sigmoid(z) + between), 0.0).astype(v.dtype)
        outs.append(jnp.einsum('bhqk,bkhd->bqhd', a, v[:, :q1]))
    return jnp.concatenate(outs, axis=1).reshape(B, S, B_WIDTH)


def pool_mixer(u, pool_w, pool_scale):
    B, S, _ = u.shape
    uf = u.astype(jnp.float32)
    cs = jnp.concatenate([jnp.zeros((B, 1, C_WIDTH), jnp.float32), jnp.cumsum(uf, axis=1)], axis=1)
    pos = jnp.arange(S, dtype=jnp.float32)
    groups = []
    for g, w in enumerate(POOL_WINDOWS):
        c = cs[..., g * POOL_CH:(g + 1) * POOL_CH]
        lag = jnp.pad(c[:, :S - w + 1], ((0, 0), (w - 1, 0), (0, 0)))
        count = jnp.minimum(pos + 1.0, float(w))[None, :, None]
        mean = (c[:, 1:] - lag) / count
        groups.append(mean - uf[..., g * POOL_CH:(g + 1) * POOL_CH])
    d = jnp.stack(groups, axis=2).astype(u.dtype)
    y = jnp.einsum('bsgc,gcd->bsgd', d, pool_w).reshape(B, S, C_WIDTH)
    return y * pool_scale


def hier_moe(h, w_group, b_group, w_expert, b_expert, w_gate, w_up, w_down):
    B, S, D = h.shape
    t = h.reshape(B * S, D)
    g_prob = jax.nn.softmax((t @ w_group + b_group).astype(jnp.float32), axis=-1)
    p_sel, g_idx = lax.top_k(g_prob, 1)
    p_sel, g_idx = p_sel[:, 0], g_idx[:, 0]
    e_logits = (t @ w_expert + b_expert).astype(jnp.float32).reshape(B * S, N_GROUPS, EXPERTS_PER_GROUP)
    e_in = jnp.take_along_axis(e_logits, g_idx[:, None, None], axis=1)[:, 0]
    e_prob = jax.nn.softmax(e_in, axis=-1)
    w_k, i_k = lax.top_k(e_prob, TOP_K_INNER)
    w_k = w_k / jnp.sum(w_k, axis=-1, keepdims=True)
    inner = jnp.sum(jax.nn.one_hot(i_k, EXPERTS_PER_GROUP, dtype=jnp.float32) * w_k[..., None], axis=1)
    combine = (p_sel[:, None, None] * jax.nn.one_hot(g_idx, N_GROUPS, dtype=jnp.float32)[:, :, None]
               * inner[:, None, :]).astype(t.dtype)
    y = jnp.zeros_like(t)
    for g in range(N_GROUPS):
        a = jnp.einsum('nd,edf->nef', t, w_gate[g])
        b = jnp.einsum('nd,edf->nef', t, w_up[g])
        hid = jax.nn.silu(a) * b * combine[:, g, :, None]
        y = y + jnp.einsum('nef,efd->nd', hid, w_down[g])
    return y.reshape(B, S, D)


def hybrid_layer(x, norm1_g, w_in, q_norm_g, w_uq, kv_norm_g, w_ukv, pool_w, pool_scale, w_out,
                 norm2_g, w_group, b_group, w_expert, b_expert, w_gate, w_up, w_down):
    B, S, _ = x.shape
    hn = rms_norm(x, norm1_g)
    proj = jnp.einsum('bsd,df->bsf', hn, w_in)
    o1 = A_QLORA
    o2 = o1 + A_KVLORA
    o3 = o2 + A_ROPE
    o4 = o3 + B_IN
    ya = mla_mixer(proj[..., :o1], proj[..., o1:o2], proj[..., o2:o3], q_norm_g, w_uq, kv_norm_g, w_ukv)
    qkv_b = proj[..., o3:o4].reshape(B, S, 3, B_HEADS, B_DIM)
    yb = stick_breaking_mixer(qkv_b[:, :, 0], qkv_b[:, :, 1], qkv_b[:, :, 2])
    yc = pool_mixer(proj[..., o4:], pool_w, pool_scale)
    mixed = jnp.concatenate([ya, yb, yc], axis=-1)
    x = x + jnp.einsum('bsm,md->bsd', mixed, w_out)
    x = x + hier_moe(rms_norm(x, norm2_g), w_group, b_group, w_expert, b_expert, w_gate, w_up, w_down)
    return x


def setup_inputs(seed: int = 0) -> dict:
    key = jax.random.key(seed)
    ks = jax.random.split(key, 20)
    L, D = DEPTH, D_MODEL

    def nrm(k, shape, fan_in):
        return jax.random.normal(k, shape, jnp.float32) * (fan_in ** -0.5)

    def gain(k, shape):
        return 1.0 + 0.02 * jax.random.normal(k, shape, jnp.float32)

    return {
        'x': jax.random.normal(ks[0], (BATCH, SEQ, D), jnp.float32),
        'norm1_g': gain(ks[1], (L, D)),
        'w_in': nrm(ks[2], (L, D, D_IN), D),
        'q_norm_g': gain(ks[3], (L, A_QLORA)),
        'w_uq': nrm(ks[4], (L, A_QLORA, A_HEADS * (A_NOPE + A_ROPE)), A_QLORA),
        'kv_norm_g': gain(ks[5], (L, A_KVLORA)),
        'w_ukv': nrm(ks[6], (L, A_KVLORA, A_HEADS * (A_NOPE + A_VDIM)), A_KVLORA),
        'pool_w': nrm(ks[7], (L, len(POOL_WINDOWS), POOL_CH, POOL_CH), POOL_CH),
        'pool_scale': gain(ks[8], (L, C_WIDTH)),
        'w_out': nrm(ks[9], (L, D_MIX, D), D_MIX),
        'norm2_g': gain(ks[10], (L, D)),
        'w_group': nrm(ks[11], (L, D, N_GROUPS), D),
        'b_group': 0.01 * jax.random.normal(ks[12], (L, N_GROUPS), jnp.float32),
        'w_expert': nrm(ks[13], (L, D, N_GROUPS * EXPERTS_PER_GROUP), D),
        'b_expert': 0.01 * jax.random.normal(ks[14], (L, N_GROUPS * EXPERTS_PER_GROUP), jnp.float32),
        'w_gate': nrm(ks[15], (L, N_GROUPS, EXPERTS_PER_GROUP, D, D_EXPERT), D),
        'w_up': nrm(ks[16], (L, N_GROUPS, EXPERTS_PER_GROUP, D, D_EXPERT), D),
        'w_down': nrm(ks[17], (L, N_GROUPS, EXPERTS_PER_GROUP, D_EXPERT, D), D_EXPERT),
        'final_g': gain(ks[18], (D,)),
    }


def reference(x, norm1_g, w_in, q_norm_g, w_uq, kv_norm_g, w_ukv, pool_w, pool_scale, w_out,
              norm2_g, w_group, b_group, w_expert, b_expert, w_gate, w_up, w_down, final_g):
    for l in range(DEPTH):
        x = hybrid_layer(x, norm1_g[l], w_in[l], q_norm_g[l], w_uq[l], kv_norm_g[l], w_ukv[l],
                         pool_w[l], pool_scale[l], w_out[l], norm2_g[l], w_group[l], b_group[l],
                         w_expert[l], b_expert[l], w_gate[l], w_up[l], w_down[l])
    return rms_norm(x, final_g)
```

```python
import functools

import jax
import jax.numpy as jnp
from jax import lax
from jax.experimental import pallas as pl
from jax.experimental.pallas import tpu as pltpu

F32 = jnp.float32
BF16 = jnp.bfloat16

D_MODEL = 1024
CHUNK = 64
EPS = 1e-6

A_HEADS = 8
A_NOPE = 64
A_ROPE = 32
A_VDIM = 64
A_QLORA = 256
A_KVLORA = 128
ROPE_THETA = 10000.0
A_HEAD_PAD = 128

B_HEADS = 4
B_DIM = 64
B_WIDTH = B_HEADS * B_DIM

POOL_WINDOWS = (2, 4, 8, 16)
POOL_CH = 64
C_WIDTH = len(POOL_WINDOWS) * POOL_CH
A_WIDTH = A_HEADS * A_VDIM

N_GROUPS = 4
EXPERTS_PER_GROUP = 8
N_EXPERTS = N_GROUPS * EXPERTS_PER_GROUP
D_EXPERT = 256

LANES = 128
D_IN_PAD = A_QLORA + A_KVLORA + LANES + 3 * B_WIDTH + C_WIDTH

PROJ_ROWS = 512
ATTN_TILE = 256
MOE_ROWS = 256
COMBINE_ROWS = 256
POOL_HALO = 16
VMEM_LIMIT = 48 * 1024 * 1024
SB_DEAD_LOG = -104.0


def _cparams(*sem):
    return pltpu.CompilerParams(dimension_semantics=sem, vmem_limit_bytes=VMEM_LIMIT)


def _rms(x, g):
    return x * lax.rsqrt(jnp.mean(x * x, axis=-1, keepdims=True) + EPS) * g


def _rope_lanes(x, cos, sin_signed, first_half):
    partner = jnp.where(first_half, pltpu.roll(x, LANES - A_ROPE // 2, 1), pltpu.roll(x, A_ROPE // 2, 1))
    return x * cos + partner * sin_signed


def _proj_kernel(x_ref, g1_ref, win_ref, qg_ref, wuq_ref, kvg_ref, wuk_ref, wuv_ref, cos_ref, sin_ref,
                 qa_ref, ka_ref, va_ref, qb_ref, kb_ref, vb_ref, u_ref):
    hn = _rms(x_ref[...], g1_ref[...]).astype(BF16)
    proj = jnp.dot(hn, win_ref[...], preferred_element_type=F32)
    o_kv = A_QLORA
    o_kr = o_kv + A_KVLORA
    o_b = o_kr + LANES
    cqn = _rms(proj[:, :o_kv], qg_ref[...]).astype(BF16)
    ckvn = _rms(proj[:, o_kv:o_kr], kvg_ref[...]).astype(BF16)
    q = jnp.dot(cqn, wuq_ref[...], preferred_element_type=F32)
    kn = jnp.dot(ckvn, wuk_ref[...], preferred_element_type=F32)
    va_ref[...] = jnp.dot(ckvn, wuv_ref[...], preferred_element_type=F32).astype(BF16)

    cos = cos_ref[...]
    sin = sin_ref[...]
    lane = lax.broadcasted_iota(jnp.int32, cos.shape, 1)
    first_half = lane < A_NOPE + A_ROPE // 2
    kr = _rope_lanes(proj[:, o_kr:o_b], cos, sin, first_half)
    scale = (A_NOPE + A_ROPE) ** -0.5
    for h in range(A_HEADS):
        sl = slice(h * A_HEAD_PAD, (h + 1) * A_HEAD_PAD)
        qa_ref[:, sl] = (_rope_lanes(q[:, sl], cos, sin, first_half) * scale).astype(BF16)
        ka_ref[:, sl] = (kn[:, sl] + kr).astype(BF16)

    qb_ref[...] = (proj[:, o_b:o_b + B_WIDTH] * (B_DIM ** -0.5)).astype(BF16)
    kb_ref[...] = proj[:, o_b + B_WIDTH:o_b + 2 * B_WIDTH].astype(BF16)
    vb_ref[...] = proj[:, o_b + 2 * B_WIDTH:o_b + 3 * B_WIDTH].astype(BF16)
    u_ref[...] = proj[:, o_b + 3 * B_WIDTH:]


def _proj(x, g1, win, qg, wuq, kvg, wuk, wuv, cos_t, sin_t, seq):
    n = x.shape[0]
    tm = PROJ_ROWS
    steps_per_seq = seq // tm
    row = lambda i: (i, 0)
    const = lambda i: (0, 0)
    full = lambda a: pl.BlockSpec(a.shape, const)
    out_w = (A_HEADS * A_HEAD_PAD, A_HEADS * A_HEAD_PAD, A_WIDTH, B_WIDTH, B_WIDTH, B_WIDTH, C_WIDTH)
    out_dt = (BF16, BF16, BF16, BF16, BF16, BF16, F32)
    return pl.pallas_call(
        _proj_kernel,
        grid=(n // tm,),
        in_specs=[pl.BlockSpec((tm, D_MODEL), row), full(g1), full(win), full(qg), full(wuq), full(kvg), full(wuk),
                  full(wuv),
                  pl.BlockSpec((tm, LANES), lambda i: (i % steps_per_seq, 0)),
                  pl.BlockSpec((tm, LANES), lambda i: (i % steps_per_seq, 0))],
        out_specs=[pl.BlockSpec((tm, w), row) for w in out_w],
        out_shape=[jax.ShapeDtypeStruct((n, w), dt) for w, dt in zip(out_w, out_dt)],
        compiler_params=_cparams("parallel"),
        name="proj",
    )(x, g1, win, qg, wuq, kvg, wuk, wuv, cos_t, sin_t)


def _mla_kernel(q_ref, k_ref, v_ref, o_ref):
    t = ATTN_TILE
    qi = pl.program_id(2)
    row = lax.broadcasted_iota(jnp.int32, (t, t), 0)
    col = lax.broadcasted_iota(jnp.int32, (t, t), 1)
    diag_ok = (col // CHUNK) <= (row // CHUNK)
    qs = [q_ref[:, j * A_HEAD_PAD:(j + 1) * A_HEAD_PAD] for j in range(2)]

    def scores(j, start):
        kt = k_ref[pl.ds(start, t), j * A_HEAD_PAD:(j + 1) * A_HEAD_PAD]
        return lax.dot_general(qs[j], kt, (((1,), (1,)), ((), ())), preferred_element_type=F32)

    q0 = pl.multiple_of(qi * t, t)
    vd = v_ref[pl.ds(q0, t), :]
    carry = []
    for j in range(2):
        s = jnp.where(diag_ok, scores(j, q0), -1e30)
        m = jnp.max(s, axis=-1, keepdims=True)
        p = jnp.exp(s - m)
        l = jnp.sum(p, axis=-1, keepdims=True)
        acc = jnp.dot(p.astype(BF16), vd, preferred_element_type=F32)
        carry += [m, l, acc]

    def body(kt_i, carry):
        start = pl.multiple_of(kt_i * t, t)
        vt = v_ref[pl.ds(start, t), :]
        out = []
        for j in range(2):
            m, l, acc = carry[3 * j:3 * j + 3]
            s = scores(j, start)
            m_new = jnp.maximum(m, jnp.max(s, axis=-1, keepdims=True))
            alpha = jnp.exp(m - m_new)
            p = jnp.exp(s - m_new)
            l = alpha * l + jnp.sum(p, axis=-1, keepdims=True)
            acc = alpha * acc + jnp.dot(p.astype(BF16), vt, preferred_element_type=F32)
            out += [m_new, l, acc]
        return tuple(out)

    carry = lax.fori_loop(0, qi, body, tuple(carry))
    lane = lax.broadcasted_iota(jnp.int32, (t, LANES), 1)
    o_ref[...] = jnp.where(lane < A_VDIM, carry[2] / carry[1], carry[5] / carry[4]).astype(BF16)


def _mla(qa, ka, va, batch, seq):
    n = qa.shape[0]
    t = ATTN_TILE
    qt = seq // t
    pairs = A_HEADS // 2
    return pl.pallas_call(
        _mla_kernel,
        grid=(batch, pairs, qt),
        in_specs=[pl.BlockSpec((t, 2 * A_HEAD_PAD), lambda b, p, i: (b * qt + i, p)),
                  pl.BlockSpec((seq, 2 * A_HEAD_PAD), lambda b, p, i: (b, p)),
                  pl.BlockSpec((seq, 2 * A_VDIM), lambda b, p, i: (b, p))],
        out_specs=pl.BlockSpec((t, 2 * A_VDIM), lambda b, p, i: (b * qt + i, p)),
        out_shape=jax.ShapeDtypeStruct((n, A_WIDTH), BF16),
        compiler_params=_cparams("parallel", "parallel", "parallel"),
        name="mla",
    )(qa, ka, va)


def _sb_kernel(q_ref, k_ref, v_ref, o_ref):
    t = ATTN_TILE
    qi = pl.program_id(2)
    lane = lax.broadcasted_iota(jnp.int32, (t, LANES), 1)
    q = q_ref[...]
    qs = [jnp.where(lane < B_DIM, q, jnp.zeros_like(q)), jnp.where(lane >= B_DIM, q, jnp.zeros_like(q))]
    row = lax.broadcasted_iota(jnp.int32, (t, t), 0)
    col = lax.broadcasted_iota(jnp.int32, (t, t), 1)
    strict = col < row
    after = jnp.where(row > col, 1.0, 0.0).astype(BF16)

    def tile(start, carried, accs, masked):
        kt = k_ref[pl.ds(start, t), :]
        vt = v_ref[pl.ds(start, t), :]
        new_c, new_a = [], []
        for j in range(2):
            z = lax.dot_general(qs[j], kt, (((1,), (1,)), ((), ())), preferred_element_type=F32)
            nsp = -(jnp.maximum(z, 0.0) + jnp.log(1.0 + jnp.exp(-jnp.abs(z))))
            lb = jnp.where(strict, nsp, 0.0) if masked else nsp
            hi = lb.astype(BF16)
            lo = (lb - hi.astype(F32)).astype(BF16)
            suffix = (jnp.dot(hi, after, preferred_element_type=F32)
                      + jnp.dot(lo, after, preferred_element_type=F32))
            w = jnp.exp(z + nsp + suffix + carried[j])
            if masked:
                w = jnp.where(strict, w, 0.0)
            new_a.append(accs[j] + jnp.dot(w.astype(BF16), vt, preferred_element_type=F32))
            new_c.append(carried[j] + jnp.sum(lb, axis=-1, keepdims=True))
        return new_c, new_a

    def alive(carried):
        return (jnp.maximum(jnp.max(carried[0]), jnp.max(carried[1])) > SB_DEAD_LOG).astype(jnp.int32)

    zc = jnp.zeros((t, 1), F32)
    za = jnp.zeros((t, LANES), F32)
    carried, accs = tile(pl.multiple_of(qi * t, t), [zc, zc], [za, za], True)

    def cond(state):
        return jnp.logical_and(state[0] >= 0, state[1] > 0)

    def body(state):
        kt_i = state[0]
        carried, accs = tile(pl.multiple_of(kt_i * t, t), list(state[2:4]), list(state[4:6]), False)
        return (kt_i - 1, alive(carried), *carried, *accs)

    state = lax.while_loop(cond, body, (qi - 1, alive(carried), *carried, *accs))
    o_ref[...] = jnp.where(lane < B_DIM, state[4], state[5]).astype(BF16)


def _stickbreak(qb, kb, vb, batch, seq):
    n = qb.shape[0]
    t = ATTN_TILE
    qt = seq // t
    pairs = B_HEADS // 2
    return pl.pallas_call(
        _sb_kernel,
        grid=(batch, pairs, qt),
        in_specs=[pl.BlockSpec((t, LANES), lambda b, p, i: (b * qt + i, p)),
                  pl.BlockSpec((seq, LANES), lambda b, p, i: (b, p)),
                  pl.BlockSpec((seq, LANES), lambda b, p, i: (b, p))],
        out_specs=pl.BlockSpec((t, LANES), lambda b, p, i: (b * qt + i, p)),
        out_shape=jax.ShapeDtypeStruct((n, B_WIDTH), BF16),
        compiler_params=_cparams("parallel", "parallel", "parallel"),
        name="stickbrk",
    )(qb, kb, vb)


def _outproj_kernel(ya_ref, yb_ref, u_ref, uprev_ref, x_ref, poolw_ref, pscale_ref, wo_ref, g2_ref,
                    wr_hi_ref, wr_lo_ref, br_ref,
                    x1_ref, h_ref, ri_ref, rf_ref, cnt_ref,
                    ext_ref, s2_ref, s4_ref, s8_ref, count_ref, *, seq):
    tm = PROJ_ROWS
    i = pl.program_id(0)
    base = (i * tm) % seq

    u = u_ref[...]
    top = 2 * POOL_HALO
    ext_ref[0:POOL_HALO, :] = jnp.zeros((POOL_HALO, C_WIDTH), F32)
    ext_ref[POOL_HALO:top, :] = jnp.where(base > 0, uprev_ref[...], 0.0)
    ext_ref[top:, :] = u
    ln = tm + top
    s2_ref[8:ln, :] = ext_ref[8:ln, :] + ext_ref[7:ln - 1, :]
    s4_ref[16:ln, :] = s2_ref[16:ln, :] + s2_ref[14:ln - 2, :]
    s8_ref[24:ln, :] = s4_ref[24:ln, :] + s4_ref[20:ln - 4, :]
    s16 = s8_ref[top:ln, :] + s8_ref[top - 8:ln - 8, :]
    clane = lax.broadcasted_iota(jnp.int32, (tm, C_WIDTH), 1)
    grp = clane // POOL_CH
    wsum = jnp.where(grp == 0, s2_ref[top:ln, :],
                     jnp.where(grp == 1, s4_ref[top:ln, :], jnp.where(grp == 2, s8_ref[top:ln, :], s16)))
    window = jnp.where(grp == 0, 2.0, jnp.where(grp == 1, 4.0, jnp.where(grp == 2, 8.0, 16.0)))
    pos = (base + lax.broadcasted_iota(jnp.int32, (tm, C_WIDTH), 0)).astype(F32)
    d = wsum / jnp.minimum(pos + 1.0, window) - u
    yc = (jnp.dot(d.astype(BF16), poolw_ref[...], preferred_element_type=F32) * pscale_ref[...]).astype(BF16)

    o = (jnp.dot(ya_ref[...], wo_ref[0:A_WIDTH, :], preferred_element_type=F32)
         + jnp.dot(yb_ref[...], wo_ref[A_WIDTH:A_WIDTH + B_WIDTH, :], preferred_element_type=F32)
         + jnp.dot(yc, wo_ref[A_WIDTH + B_WIDTH:, :], preferred_element_type=F32))
    x1 = x_ref[...] + o
    x1_ref[...] = x1
    h = _rms(x1, g2_ref[...])
    h_ref[...] = h

    h_hi = h.astype(BF16)
    h_lo = (h - h_hi.astype(F32)).astype(BF16)
    logits = (jnp.dot(h_hi, wr_hi_ref[...], preferred_element_type=F32)
              + jnp.dot(h_lo, wr_hi_ref[...], preferred_element_type=F32)
              + jnp.dot(h_hi, wr_lo_ref[...], preferred_element_type=F32)) + br_ref[...]
    lane = lax.broadcasted_iota(jnp.int32, (tm, LANES), 1).astype(F32)
    neg = -jnp.inf
    far = float(LANES)
    is_g = lane < N_GROUPS
    gl = jnp.where(is_g, logits, neg)
    gmax = jnp.max(gl, axis=-1, keepdims=True)
    g_idx = jnp.min(jnp.where(gl == gmax, lane, far), axis=-1, keepdims=True)
    p_sel = 1.0 / jnp.sum(jnp.where(is_g, jnp.exp(logits - gmax), 0.0), axis=-1, keepdims=True)
    lo_lane = N_GROUPS + EXPERTS_PER_GROUP * g_idx
    in_grp = jnp.logical_and(lane >= lo_lane, lane < lo_lane + EXPERTS_PER_GROUP)
    el = jnp.where(in_grp, logits, neg)
    m1 = jnp.max(el, axis=-1, keepdims=True)
    i1 = jnp.min(jnp.where(el == m1, lane, far), axis=-1, keepdims=True)
    el2 = jnp.where(lane == i1, neg, el)
    m2 = jnp.max(el2, axis=-1, keepdims=True)
    i2 = jnp.min(jnp.where(el2 == m2, lane, far), axis=-1, keepdims=True)
    ratio = jnp.exp(m2 - m1)
    w1 = 1.0 / (1.0 + ratio)
    c1 = p_sel * w1
    c2 = p_sel * (ratio * w1)
    e1 = i1 - N_GROUPS
    e2 = i2 - N_GROUPS

    @pl.when(i == 0)
    def _():
        count_ref[...] = jnp.zeros_like(count_ref)

    r = lax.broadcasted_iota(jnp.int32, (tm, tm), 0)
    c = lax.broadcasted_iota(jnp.int32, (tm, tm), 1)
    upto = jnp.where(c <= r, 1.0, 0.0).astype(BF16)
    oh1 = jnp.where(lane == e1, 1.0, 0.0)
    oh2 = jnp.where(lane == e2, 1.0, 0.0)
    cum1 = jnp.dot(upto, oh1.astype(BF16), preferred_element_type=F32)
    cum2 = jnp.dot(upto, oh2.astype(BF16), preferred_element_type=F32)
    tot1 = jnp.sum(oh1, axis=0, keepdims=True)
    tot2 = jnp.sum(oh2, axis=0, keepdims=True)
    before = count_ref[...]
    rank1 = jnp.sum(oh1 * (cum1 - 1.0 + before), axis=-1, keepdims=True)
    rank2 = jnp.sum(oh2 * (cum2 - 1.0 + before + tot1), axis=-1, keepdims=True)
    count = before + tot1 + tot2
    count_ref[...] = count
    cnt_ref[...] = jnp.broadcast_to(count.reshape(1, 1, LANES), cnt_ref.shape)

    ri = jnp.where(lane == 0.0, e1, jnp.where(lane == 1.0, e2, jnp.where(lane == 2.0, rank1,
                                                                           jnp.where(lane == 3.0, rank2, 0.0))))
    ri_ref[...] = ri.astype(jnp.int32)
    rf_ref[...] = jnp.where(lane == 0.0, c1, jnp.where(lane == 1.0, c2, 0.0))


def _outproj(ya, yb, u, x, poolw, pscale, wo, g2, wr_hi, wr_lo, br, seq):
    n = x.shape[0]
    tm = PROJ_ROWS
    nt = n // tm
    halo_blocks = tm // POOL_HALO
    row = lambda i: (i, 0)
    const = lambda i: (0, 0)
    full = lambda a: pl.BlockSpec(a.shape, const)
    ext_rows = tm + 2 * POOL_HALO
    return pl.pallas_call(
        functools.partial(_outproj_kernel, seq=seq),
        grid=(nt,),
        in_specs=[pl.BlockSpec((tm, A_WIDTH), row), pl.BlockSpec((tm, B_WIDTH), row),
                  pl.BlockSpec((tm, C_WIDTH), row),
                  pl.BlockSpec((POOL_HALO, C_WIDTH), lambda i: (jnp.maximum(i * halo_blocks - 1, 0), 0)),
                  pl.BlockSpec((tm, D_MODEL), row),
                  full(poolw), full(pscale), full(wo), full(g2), full(wr_hi), full(wr_lo), full(br)],
        out_specs=[pl.BlockSpec((tm, D_MODEL), row), pl.BlockSpec((tm, D_MODEL), row),
                   pl.BlockSpec((tm, LANES), row), pl.BlockSpec((tm, LANES), row),
                   pl.BlockSpec((1, 8, LANES), lambda i: (i, 0, 0))],
        out_shape=[jax.ShapeDtypeStruct((n, D_MODEL), F32), jax.ShapeDtypeStruct((n, D_MODEL), F32),
                   jax.ShapeDtypeStruct((n, LANES), jnp.int32), jax.ShapeDtypeStruct((n, LANES), F32),
                   jax.ShapeDtypeStruct((nt, 8, LANES), F32)],
        scratch_shapes=[pltpu.VMEM((ext_rows, C_WIDTH), F32)] * 4 + [pltpu.VMEM((1, LANES), F32)],
        compiler_params=_cparams("arbitrary"),
        name="outproj",
    )(ya, yb, u, u, x, poolw, pscale, wo, g2, wr_hi, wr_lo, br)


def _row_copy(src, src_row, dst, dst_row, sem):
    return pltpu.make_async_copy(src.at[pl.ds(src_row, 1)], dst.at[pl.ds(dst_row, 1)], sem)


def _dispatch_kernel(slot_ref, h_ref, hs_ref, sem):
    tm = PROJ_ROWS

    def issue(r, _):
        _row_copy(h_ref, r, hs_ref, slot_ref[r], sem).start()
        _row_copy(h_ref, r, hs_ref, slot_ref[tm + r], sem).start()
        return 0

    lax.fori_loop(0, tm, issue, 0)

    def drain(r, _):
        _row_copy(h_ref, r, hs_ref, slot_ref[r], sem).wait()
        _row_copy(h_ref, r, hs_ref, slot_ref[tm + r], sem).wait()
        return 0

    lax.fori_loop(0, tm, drain, 0)


def _dispatch(slots_tiled, h, rows_out):
    n = h.shape[0]
    tm = PROJ_ROWS
    return pl.pallas_call(
        _dispatch_kernel,
        grid=(n // tm,),
        in_specs=[pl.BlockSpec((2 * tm,), lambda i: (i,), memory_space=pltpu.SMEM),
                  pl.BlockSpec((tm, D_MODEL), lambda i: (i, 0))],
        out_specs=pl.BlockSpec(memory_space=pl.ANY),
        out_shape=jax.ShapeDtypeStruct((rows_out, D_MODEL), F32),
        scratch_shapes=[pltpu.SemaphoreType.DMA(())],
        compiler_params=_cparams("arbitrary"),
        name="dispatch",
    )(slots_tiled, h)


def _moe_kernel(te_ref, tv_ref, nu_ref, hs_ref, wg_ref, wu_ref, wd_ref, ys_ref):
    i = pl.program_id(0)

    @pl.when(i < nu_ref[0])
    def _():
        t = MOE_ROWS
        rowid = lax.broadcasted_iota(jnp.int32, (t, D_MODEL), 0)
        x = jnp.where(rowid < tv_ref[i], hs_ref[...], 0.0).astype(BF16)
        a = jnp.dot(x, wg_ref[0].astype(BF16), preferred_element_type=F32)
        b = jnp.dot(x, wu_ref[0].astype(BF16), preferred_element_type=F32)
        hid = (a * jax.nn.sigmoid(a) * b).astype(BF16)
        ys_ref[...] = jnp.dot(hid, wd_ref[0].astype(BF16), preferred_element_type=F32)


def _moe(tile_expert, tile_valid, n_used, hs, wg, wu, wd):
    rows = hs.shape[0]
    t = MOE_ROWS
    nt = rows // t

    def tile_of(i, te, tv, nu):
        return jnp.minimum(i, nu[0] - 1)

    def wspec(shape):
        return pl.BlockSpec((1,) + shape, lambda i, te, tv, nu: (te[tile_of(i, te, tv, nu)], 0, 0))

    return pl.pallas_call(
        _moe_kernel,
        grid_spec=pltpu.PrefetchScalarGridSpec(
            num_scalar_prefetch=3,
            grid=(nt,),
            in_specs=[pl.BlockSpec((t, D_MODEL), lambda i, te, tv, nu: (tile_of(i, te, tv, nu), 0)),
                      wspec((D_MODEL, D_EXPERT)), wspec((D_MODEL, D_EXPERT)), wspec((D_EXPERT, D_MODEL))],
            out_specs=pl.BlockSpec((t, D_MODEL), lambda i, te, tv, nu: (tile_of(i, te, tv, nu), 0)),
        ),
        out_shape=jax.ShapeDtypeStruct((rows, D_MODEL), F32),
        compiler_params=_cparams("arbitrary"),
        name="moe",
    )(tile_expert, tile_valid, n_used, hs, wg, wu, wd)


def _combine_kernel(slot_ref, x1_ref, rf_ref, fg_ref, ys_ref, o_ref, buf_ref, sem, *, final):
    tm = COMBINE_ROWS

    def issue(r, _):
        _row_copy(ys_ref, slot_ref[r], buf_ref.at[0], r, sem).start()
        _row_copy(ys_ref, slot_ref[tm + r], buf_ref.at[1], r, sem).start()
        return 0

    lax.fori_loop(0, tm, issue, 0)

    def drain(r, _):
        _row_copy(ys_ref, slot_ref[r], buf_ref.at[0], r, sem).wait()
        _row_copy(ys_ref, slot_ref[tm + r], buf_ref.at[1], r, sem).wait()
        return 0

    lax.fori_loop(0, tm, drain, 0)
    rf = rf_ref[...]
    x2 = x1_ref[...] + rf[:, 0:1] * buf_ref[0] + rf[:, 1:2] * buf_ref[1]
    if final:
        x2 = _rms(x2, fg_ref[...])
    o_ref[...] = x2


def _combine(slots_tiled, x1, rf, fg, ys, final):
    n = x1.shape[0]
    tm = COMBINE_ROWS
    return pl.pallas_call(
        functools.partial(_combine_kernel, final=final),
        grid=(n // tm,),
        in_specs=[pl.BlockSpec((2 * tm,), lambda i: (i,), memory_space=pltpu.SMEM),
                  pl.BlockSpec((tm, D_MODEL), lambda i: (i, 0)),
                  pl.BlockSpec((tm, LANES), lambda i: (i, 0)),
                  pl.BlockSpec((1, D_MODEL), lambda i: (0, 0)),
                  pl.BlockSpec(memory_space=pl.ANY)],
        out_specs=pl.BlockSpec((tm, D_MODEL), lambda i: (i, 0)),
        out_shape=jax.ShapeDtypeStruct((n, D_MODEL), F32),
        scratch_shapes=[pltpu.VMEM((2, tm, D_MODEL), F32), pltpu.SemaphoreType.DMA(())],
        compiler_params=_cparams("arbitrary"),
        name="combine",
    )(slots_tiled, x1, rf, fg, ys)


def _rope_tables(seq):
    pos = jnp.arange(seq, dtype=F32)
    inv = ROPE_THETA ** (-jnp.arange(0, A_ROPE, 2, dtype=F32) / A_ROPE)
    ang = pos[:, None] * inv[None, :]
    cos, sin = jnp.cos(ang), jnp.sin(ang)
    half = A_ROPE // 2
    cos_t = jnp.ones((seq, LANES), F32).at[:, A_NOPE:A_NOPE + half].set(cos).at[:, A_NOPE + half:A_NOPE + A_ROPE].set(cos)
    sin_t = jnp.zeros((seq, LANES), F32).at[:, A_NOPE:A_NOPE + half].set(-sin).at[:, A_NOPE + half:A_NOPE + A_ROPE].set(sin)
    return cos_t, sin_t


def _layer_weights(w_in, w_uq, w_ukv, pool_w, w_group, w_expert, b_group, b_expert):
    o1 = A_QLORA
    o2 = o1 + A_KVLORA
    o3 = o2 + A_ROPE
    kr_pad = jnp.zeros((D_MODEL, LANES), F32).at[:, A_NOPE:A_NOPE + A_ROPE].set(w_in[:, o2:o3])
    win = jnp.concatenate([w_in[:, :o2], kr_pad, w_in[:, o3:]], axis=1).astype(BF16)
    qk = A_NOPE + A_ROPE
    wuq = jnp.pad(w_uq.reshape(A_QLORA, A_HEADS, qk), ((0, 0), (0, 0), (0, A_HEAD_PAD - qk)))
    wuq = wuq.reshape(A_QLORA, A_HEADS * A_HEAD_PAD).astype(BF16)
    wkv = w_ukv.reshape(A_KVLORA, A_HEADS, A_NOPE + A_VDIM)
    wuk = jnp.pad(wkv[:, :, :A_NOPE], ((0, 0), (0, 0), (0, A_HEAD_PAD - A_NOPE)))
    wuk = wuk.reshape(A_KVLORA, A_HEADS * A_HEAD_PAD).astype(BF16)
    wuv = wkv[:, :, A_NOPE:].reshape(A_KVLORA, A_WIDTH).astype(BF16)
    poolw = jax.scipy.linalg.block_diag(*[pool_w[g] for g in range(len(POOL_WINDOWS))]).astype(BF16)
    wr = jnp.zeros((D_MODEL, LANES), F32).at[:, :N_GROUPS].set(w_group).at[:, N_GROUPS:N_GROUPS + N_EXPERTS].set(w_expert)
    wr_hi = wr.astype(BF16)
    wr_lo = (wr - wr_hi.astype(F32)).astype(BF16)
    br = jnp.zeros((1, LANES), F32).at[0, :N_GROUPS].set(b_group).at[0, N_GROUPS:N_GROUPS + N_EXPERTS].set(b_expert)
    return win, wuq, wuk, wuv, poolw, wr_hi, wr_lo, br


def _routing_tables(ri, cnt, n_tiles):
    t = MOE_ROWS
    counts = cnt[-1, 0, :N_EXPERTS].astype(jnp.int32)
    padded = ((counts + t - 1) // t) * t
    ends = jnp.cumsum(padded)
    off = ends - padded
    slot1 = off[ri[:, 0]] + ri[:, 2]
    slot2 = off[ri[:, 1]] + ri[:, 3]
    n_used = ends[-1] // t
    tile_start = jnp.minimum(jnp.arange(n_tiles, dtype=jnp.int32), n_used - 1) * t
    tile_expert = jnp.sum((ends[None, :] <= tile_start[:, None]).astype(jnp.int32), axis=1)
    tile_expert = jnp.minimum(tile_expert, N_EXPERTS - 1)
    tile_valid = jnp.clip(counts[tile_expert] - (tile_start - off[tile_expert]), 0, t).astype(jnp.int32)
    return slot1, slot2, tile_expert, tile_valid, n_used.reshape(1).astype(jnp.int32)


def _tile_slots(slot1, slot2, tm):
    n = slot1.shape[0]
    return jnp.concatenate([slot1.reshape(n // tm, tm), slot2.reshape(n // tm, tm)], axis=1).reshape(-1)


def kernel(x, norm1_g, w_in, q_norm_g, w_uq, kv_norm_g, w_ukv, pool_w, pool_scale, w_out, norm2_g, w_group, b_group,
           w_expert, b_expert, w_gate, w_up, w_down, final_g):
    batch, seq, _ = x.shape
    n = batch * seq
    depth = w_in.shape[0]
    assert seq % PROJ_ROWS == 0 and seq % ATTN_TILE == 0 and n % COMBINE_ROWS == 0
    rows_sorted = 2 * n + N_EXPERTS * MOE_ROWS
    n_tiles = rows_sorted // MOE_ROWS
    cos_t, sin_t = _rope_tables(seq)
    xf = x.reshape(n, D_MODEL)
    fg = final_g.reshape(1, D_MODEL)
    for l in range(depth):
        win, wuq, wuk, wuv, poolw, wr_hi, wr_lo, br = _layer_weights(
            w_in[l], w_uq[l], w_ukv[l], pool_w[l], w_group[l], w_expert[l], b_group[l], b_expert[l])
        qa, ka, va, qb, kb, vb, u = _proj(xf, norm1_g[l].reshape(1, -1), win, q_norm_g[l].reshape(1, -1), wuq,
                                          kv_norm_g[l].reshape(1, -1), wuk, wuv, cos_t, sin_t, seq)
        ya = _mla(qa, ka, va, batch, seq)
        yb = _stickbreak(qb, kb, vb, batch, seq)
        x1, h, ri, rf, cnt = _outproj(ya, yb, u, xf, poolw, pool_scale[l].reshape(1, -1), w_out[l].astype(BF16),
                                      norm2_g[l].reshape(1, -1), wr_hi, wr_lo, br, seq)
        slot1, slot2, tile_expert, tile_valid, n_used = _routing_tables(ri, cnt, n_tiles)
        hs = _dispatch(_tile_slots(slot1, slot2, PROJ_ROWS), h, rows_sorted)
        ys = _moe(tile_expert, tile_valid, n_used, hs,
                  w_gate[l].reshape(N_EXPERTS, D_MODEL, D_EXPERT), w_up[l].reshape(N_EXPERTS, D_MODEL, D_EXPERT),
                  w_down[l].reshape(N_EXPERTS, D_EXPERT, D_MODEL))
        xf = _combine(_tile_slots(slot1, slot2, COMBINE_ROWS), x1, rf, fg, ys, final=(l == depth - 1))
    return xf.reshape(batch, seq, D_MODEL)
```
